```python
import math
import jax, jax.numpy as jnp
from jax import lax
import numpy as np

D_MODEL = 1024
BATCH = 4
SEQ = 8192
DEPTH = 4
DEC_BATCH = 16
DEC_SEQ = 2048
PAST_LEN = 128

MIX_WIDTH = 1024
HG_HEADS = 5
HG_DK = 128
HG_DV = 128
HG_WIDTH = 640
HG_CHUNK = 64
ATTN_WINDOWS = (128, 512, 2048)
ATTN_DILATIONS = (1, 4, 16)
ATTN_GROUPS = 3
ATTN_HEADS_PER_GROUP = 2
ATTN_HEADS = 6
HEAD_DIM = 64
ATTN_WIDTH = 384
ROT_DIM = 16
ROPE_THETA = 500000.0
IN_COLS = 5 * 640 + 3 * 384
N_EXPERTS = 16
EXPERT_FF = 1024
EC_CAPACITY = 2
NORM_EPS = 1e-6

kernel_name = "hybrid_hgrn2_dilated_attn_ec_moe_encoder"


def rms_norm(x, w):
    xf = x.astype(jnp.float32)
    y = xf * lax.rsqrt(jnp.mean(xf * xf, axis=-1, keepdims=True) + NORM_EPS)
    return (y * w.astype(jnp.float32)).astype(x.dtype)


def partial_rotary(t, pos):
    half = ROT_DIM // 2
    inv_freq = ROPE_THETA ** (-jnp.arange(half, dtype=jnp.float32) * 2.0 / ROT_DIM)
    ang = pos.astype(jnp.float32)[:, None] * inv_freq[None, :]
    cos = jnp.cos(ang)[None, :, None, :]
    sin = jnp.sin(ang)[None, :, None, :]
    t1 = t[..., :half]
    t2 = t[..., half:ROT_DIM]
    return jnp.concatenate([t1 * cos - t2 * sin, t1 * sin + t2 * cos, t[..., ROT_DIM:]], axis=-1)


def layer_lower_bounds(logits):
    c = jnp.cumsum(jax.nn.softmax(logits.astype(jnp.float32), axis=0), axis=0)
    return c - c[0:1]


def gla_chunk_scan(q, k, v, logf):
    B, S, H, DK = q.shape
    DV = v.shape[-1]
    C = HG_CHUNK
    n = S // C

    def to_chunks(t):
        return t.reshape(B, n, C, H, t.shape[-1]).transpose(1, 0, 3, 2, 4)

    tril = jnp.tril(jnp.ones((C, C), dtype=bool))[:, :, None]

    def step(state, inp):
        q_c, k_c, v_c, g_c = inp
        b = jnp.cumsum(g_c, axis=2)
        diff = b[:, :, :, None, :] - b[:, :, None, :, :]
        decay = jnp.exp(jnp.where(tril, diff, -jnp.inf))
        scores = jnp.sum(q_c[:, :, :, None, :] * k_c[:, :, None, :, :] * decay, axis=-1)
        o = (jnp.einsum('bhts,bhse->bhte', scores, v_c)
             + jnp.einsum('bhtd,bhde->bhte', q_c * jnp.exp(b), state))
        b_last = b[:, :, -1:, :]
        new_state = (jnp.exp(b_last[:, :, 0, :])[..., None] * state
                     + jnp.einsum('bhsd,bhse->bhde', k_c * jnp.exp(b_last - b), v_c))
        return new_state, o

    init = jnp.zeros((B, H, DK, DV), jnp.float32)
    _, o = lax.scan(step, init, (to_chunks(q), to_chunks(k), to_chunks(v), to_chunks(logf)))
    return o.transpose(1, 0, 3, 2, 4).reshape(B, S, H, DV)


def hgrn2_mixer(q_raw, zf, zb, i_raw, g_raw, lb_f, lb_b, norm_w):
    B, S, _ = q_raw.shape
    dt = q_raw.dtype

    def heads(t):
        return t.astype(jnp.float32).reshape(B, S, HG_HEADS, HG_DK)

    q = jax.nn.silu(heads(q_raw))
    v = heads(i_raw)

    def log_forget(z, lb):
        lb = lb.reshape(HG_HEADS, HG_DK)
        return jnp.logaddexp(jnp.log(lb), jnp.log1p(-lb) + jax.nn.log_sigmoid(z))

    logf_f = log_forget(heads(zf), lb_f)
    logf_b = log_forget(heads(zb), lb_b)
    k_f = -jnp.expm1(logf_f)
    k_b = -jnp.expm1(logf_b)

    o_f = gla_chunk_scan(q, k_f, v, logf_f)
    flip = lambda t: jnp.flip(t, axis=1)
    o_b = flip(gla_chunk_scan(flip(q), flip(k_b), flip(v), flip(logf_b)))
    o = o_f + o_b
    o = o * lax.rsqrt(jnp.mean(o * o, axis=-1, keepdims=True) + NORM_EPS)
    o = o * norm_w.astype(jnp.float32).reshape(HG_HEADS, HG_DV)
    o = o.reshape(B, S, HG_WIDTH) * jax.nn.silu(g_raw.astype(jnp.float32))
    return o.astype(dt)


def dilated_window_attention(q, k, v, dilation, radius):
    B, S, H, hd = q.shape
    L = S // dilation
    blk = radius
    nb = -(-L // blk)
    Lp = nb * blk
    Bd = B * dilation

    def residues(t):
        return t.reshape(B, L, dilation, H, hd).transpose(0, 2, 1, 3, 4).reshape(Bd, L, H, hd)

    qs = jnp.pad(residues(q), ((0, 0), (0, Lp - L), (0, 0), (0, 0))).reshape(Bd, nb, blk, H, hd)

    def key_windows(t):
        tp = jnp.pad(residues(t), ((0, 0), (blk, Lp - L + blk), (0, 0), (0, 0))).reshape(Bd, nb + 2, blk, H, hd)
        return jnp.concatenate([tp[:, :-2], tp[:, 1:-1], tp[:, 2:]], axis=2)

    kw = key_windows(k)
    vw = key_windows(v)
    qpos = jnp.arange(nb)[:, None] * blk + jnp.arange(blk)[None, :]
    kpos = jnp.arange(nb)[:, None] * blk - blk + jnp.arange(3 * blk)[None, :]
    valid = ((jnp.abs(qpos[:, :, None] - kpos[:, None, :]) <= radius)
             & (kpos >= 0)[:, None, :] & (kpos < L)[:, None, :])
    s = jnp.einsum('bnqhd,bnkhd->bnhqk', qs, kw)
    s = jnp.where(valid[None, :, None], s, -jnp.inf)
    m = jnp.max(s, axis=-1, keepdims=True)
    p = jnp.exp(s - m)
    l = jnp.sum(p, axis=-1, keepdims=True)
    o = jnp.einsum('bnhqk,bnkhd->bnqhd', p, vw) / jnp.transpose(l, (0, 1, 3, 2, 4))
    lse = jnp.transpose((m + jnp.log(l))[..., 0], (0, 1, 3, 2))

    def back(t):
        t = t.reshape(Bd, Lp, *t.shape[3:])[:, :L]
        t = t.reshape(B, dilation, L, *t.shape[2:])
        return jnp.swapaxes(t, 1, 2).reshape(B, S, *t.shape[3:])

    return back(o), back(lse)


def dilated_attention_mixer(q_raw, k_raw, v_raw):
    B, S, _ = q_raw.shape
    dt = q_raw.dtype
    pos = jnp.arange(S)

    def heads(t):
        return t.astype(jnp.float32).reshape(B, S, ATTN_HEADS, HEAD_DIM)

    q = partial_rotary(heads(q_raw), pos) * (HEAD_DIM ** -0.5)
    k = partial_rotary(heads(k_raw), pos)
    v = heads(v_raw)
    outs, lses = [], []
    for g in range(ATTN_GROUPS):
        sl = slice(g * ATTN_HEADS_PER_GROUP, (g + 1) * ATTN_HEADS_PER_GROUP)
        radius = ATTN_WINDOWS[g] // (2 * ATTN_DILATIONS[g])
        o_g, lse_g = dilated_window_attention(q[:, :, sl], k[:, :, sl], v[:, :, sl],
                                              ATTN_DILATIONS[g], radius)
        outs.append(o_g)
        lses.append(lse_g)
    alpha = jax.nn.softmax(jnp.stack(lses, axis=0), axis=0)
    o = jnp.stack(outs, axis=0) * alpha[..., None]
    return jnp.moveaxis(o, 0, 2).reshape(B, S, ATTN_WIDTH).astype(dt)


def expert_choice_moe(x, w_router, w_gate, w_up, w_down):
    B, S, D = x.shape
    n_tok = B * S
    capacity = EC_CAPACITY * n_tok // N_EXPERTS
    xf = x.reshape(n_tok, D)
    aff = jax.nn.softmax((xf @ w_router).astype(jnp.float32), axis=-1)
    gate, idx = lax.top_k(aff.T, capacity)
    xe = xf[idx]
    h = (jax.nn.silu(jnp.einsum('ecd,edf->ecf', xe, w_gate))
         * jnp.einsum('ecd,edf->ecf', xe, w_up))
    ye = jnp.einsum('ecf,efd->ecd', h, w_down) * gate[..., None].astype(x.dtype)
    y = jnp.zeros((n_tok, D), x.dtype).at[idx.reshape(-1)].add(ye.reshape(-1, D))
    return y.reshape(B, S, D)


def trunk(x, norm1_w, w_in, lb_fwd_logits, lb_bwd_logits, hgrn_norm_w, w_out,
          norm2_w, w_router, w_gate, w_up, w_down, final_norm_w):
    lb_f_all = layer_lower_bounds(lb_fwd_logits)
    lb_b_all = layer_lower_bounds(lb_bwd_logits)
    splits = [HG_WIDTH * j for j in range(1, 6)] + [5 * HG_WIDTH + ATTN_WIDTH, 5 * HG_WIDTH + 2 * ATTN_WIDTH]
    for l in range(DEPTH):
        h = rms_norm(x, norm1_w[l])
        proj = h @ w_in[l]
        q_h, zf, zb, i_h, g_h, q_a, k_a, v_a = jnp.split(proj, splits, axis=-1)
        hg_out = hgrn2_mixer(q_h, zf, zb, i_h, g_h, lb_f_all[l], lb_b_all[l], hgrn_norm_w[l])
        at_out = dilated_attention_mixer(q_a, k_a, v_a)
        x = x + jnp.concatenate([hg_out, at_out], axis=-1) @ w_out[l]
        x = x + expert_choice_moe(rms_norm(x, norm2_w[l]), w_router[l], w_gate[l], w_up[l], w_down[l])
    return rms_norm(x, final_norm_w)


def setup_inputs(seed: int = 0) -> dict:
    key = jax.random.key(seed)
    ks = jax.random.split(key, 14)
    f32 = jnp.float32
    nrm = lambda k, shape: jax.random.normal(k, shape, f32)
    return {
        "x_prompt": nrm(ks[0], (BATCH, SEQ, D_MODEL)),
        "x_sample": nrm(ks[1], (DEC_BATCH, DEC_SEQ, D_MODEL)),
        "norm1_w": 1.0 + 0.02 * nrm(ks[2], (DEPTH, D_MODEL)),
        "w_in": nrm(ks[3], (DEPTH, D_MODEL, IN_COLS)) * D_MODEL ** -0.5,
        "lb_fwd_logits": 0.1 * nrm(ks[4], (DEPTH, HG_WIDTH)),
        "lb_bwd_logits": 0.1 * nrm(ks[5], (DEPTH, HG_WIDTH)),
        "hgrn_norm_w": 1.0 + 0.02 * nrm(ks[6], (DEPTH, HG_WIDTH)),
        "w_out": nrm(ks[7], (DEPTH, MIX_WIDTH, D_MODEL)) * MIX_WIDTH ** -0.5,
        "norm2_w": 1.0 + 0.02 * nrm(ks[8], (DEPTH, D_MODEL)),
        "w_router": nrm(ks[9], (DEPTH, D_MODEL, N_EXPERTS)) * D_MODEL ** -0.5,
        "w_gate": nrm(ks[10], (DEPTH, N_EXPERTS, D_MODEL, EXPERT_FF)) * D_MODEL ** -0.5,
        "w_up": nrm(ks[11], (DEPTH, N_EXPERTS, D_MODEL, EXPERT_FF)) * D_MODEL ** -0.5,
        "w_down": nrm(ks[12], (DEPTH, N_EXPERTS, EXPERT_FF, D_MODEL)) * EXPERT_FF ** -0.5,
        "final_norm_w": 1.0 + 0.02 * nrm(ks[13], (D_MODEL,)),
    }


def reference(x_prompt, x_sample, norm1_w, w_in, lb_fwd_logits, lb_bwd_logits, hgrn_norm_w,
              w_out, norm2_w, w_router, w_gate, w_up, w_down, final_norm_w):
    y_prompt = trunk(x_prompt, norm1_w, w_in, lb_fwd_logits, lb_bwd_logits, hgrn_norm_w, w_out,
                     norm2_w, w_router, w_gate, w_up, w_down, final_norm_w)
    y_sample = trunk(x_sample, norm1_w, w_in, lb_fwd_logits, lb_bwd_logits, hgrn_norm_w, w_out,
                     norm2_w, w_router, w_gate, w_up, w_down, final_norm_w)
    return (y_prompt, y_sample)
```

```python
import functools

import jax
import jax.numpy as jnp
from jax import lax
from jax.experimental import pallas as pl
from jax.experimental.pallas import tpu as pltpu

D_MODEL = 1024
DEPTH = 4
HG_HEADS = 5
HG_DK = 128
HG_WIDTH = 640
ATTN_DILATIONS = (1, 4, 16)
ATTN_RADIUS = 64
ATTN_GROUPS = 3
HEAD_DIM = 64
ROT_DIM = 16
ROPE_THETA = 500000.0
IN_COLS = 5 * 640 + 3 * 384
ATTN_COL0 = 5 * 640
N_EXPERTS = 16
EC_CAPACITY = 2
NORM_EPS = 1e-6

LANES = 128
COL_BLOCKS = IN_COLS // LANES
CHUNK = 128
NEG_BIG = -1e30
VMEM_LIMIT = 56 * 1024 * 1024

F32 = jnp.float32
BF16 = jnp.bfloat16

_NT = (((1,), (1,)), ((), ()))
_TN = (((0,), (0,)), ((), ()))


def _cparams(sem):
    return pltpu.CompilerParams(dimension_semantics=sem, vmem_limit_bytes=VMEM_LIMIT)


def _lb_kernel(lf_ref, lb_ref, out_ref):
    for j, ref in enumerate((lf_ref, lb_ref)):
        x = ref[...]
        m = jnp.max(x, axis=0, keepdims=True)
        e = jnp.exp(x - m)
        p = e / jnp.sum(e, axis=0, keepdims=True)
        c = p[0:1]
        rows = [c]
        for l in range(1, DEPTH):
            c = c + p[l:l + 1]
            rows.append(c)
        for l in range(DEPTH):
            lb = rows[l] - rows[0]
            out_ref[l, 2 * j:2 * j + 1, :] = jnp.log(lb)
            out_ref[l, 2 * j + 1:2 * j + 2, :] = jnp.log(1.0 - lb)


def _lower_bound_logs(lb_fwd_logits, lb_bwd_logits):
    return pl.pallas_call(
        _lb_kernel,
        out_shape=jax.ShapeDtypeStruct((DEPTH, 4, HG_WIDTH), F32),
        name="lb_logs",
    )(lb_fwd_logits, lb_bwd_logits)


def _inproj_kernel(x_ref, nw_ref, w_ref, cos_ref, sin_ref, o_ref):
    x = x_ref[...]
    ms = jnp.mean(x * x, axis=-1, keepdims=True)
    h = (x * lax.rsqrt(ms + NORM_EPS) * nw_ref[...]).astype(BF16)
    acc = jnp.dot(h, w_ref[...], preferred_element_type=F32)
    o_ref[:, :ATTN_COL0] = acc[:, :ATTN_COL0]
    cos = cos_ref[...]
    sin = sin_ref[...]
    lane = lax.broadcasted_iota(jnp.int32, cos.shape, 1) % HEAD_DIM
    half = ROT_DIM // 2
    for blk in range(2 * ATTN_GROUPS):
        a = ATTN_COL0 + blk * LANES
        t = acc[:, a:a + LANES]
        partner = jnp.where(lane < half, pltpu.roll(t, LANES - half, 1), pltpu.roll(t, half, 1))
        r = t * cos + partner * sin
        if blk < ATTN_GROUPS:
            r = r * (HEAD_DIM ** -0.5)
        o_ref[:, a:a + LANES] = r
    a = ATTN_COL0 + 2 * ATTN_GROUPS * LANES
    o_ref[:, a:] = acc[:, a:]


def _rotary_tables(seq):
    half = ROT_DIM // 2
    inv_freq = ROPE_THETA ** (-jnp.arange(half, dtype=F32) * 2.0 / ROT_DIM)
    ang = jnp.arange(seq, dtype=F32)[:, None] * inv_freq[None, :]
    lane = jnp.arange(LANES) % HEAD_DIM
    ang_l = ang[:, lane % half]
    cos = jnp.where(lane[None, :] < ROT_DIM, jnp.cos(ang_l), 1.0)
    sin = jnp.sin(ang_l)
    sin = jnp.where(lane[None, :] < half, -sin, jnp.where(lane[None, :] < ROT_DIM, sin, 0.0))
    return cos.astype(F32), sin.astype(F32)


def _inproj(x2, nw, w_bf, cos, sin, seq, tm):
    n = x2.shape[0]
    spb = seq // tm
    return pl.pallas_call(
        _inproj_kernel,
        out_shape=jax.ShapeDtypeStruct((n, IN_COLS), F32),
        grid=(n // tm,),
        in_specs=[
            pl.BlockSpec((tm, D_MODEL), lambda i: (i, 0)),
            pl.BlockSpec((1, D_MODEL), lambda i: (0, 0)),
            pl.BlockSpec((D_MODEL, IN_COLS), lambda i: (0, 0), pipeline_mode=pl.Buffered(1)),
            pl.BlockSpec((tm, LANES), lambda i: (i % spb, 0)),
            pl.BlockSpec((tm, LANES), lambda i: (i % spb, 0)),
        ],
        out_specs=pl.BlockSpec((tm, IN_COLS), lambda i: (i, 0)),
        compiler_params=_cparams(("arbitrary",)),
        name="inproj",
    )(x2, nw, w_bf, cos, sin)


def _hgrn_kernel(*refs, rev, nchunk):
    if rev:
        q_ref, z_ref, i_ref, lbp_ref, g_ref, of_ref, nw_ref, out_ref, st_ref = refs
    else:
        q_ref, z_ref, i_ref, lbp_ref, out_ref, st_ref = refs

    @pl.when(pl.program_id(1) == 0)
    def _():
        st_ref[...] = jnp.zeros_like(st_ref)

    r = lax.broadcasted_iota(jnp.int32, (CHUNK, CHUNK), 0)
    c = lax.broadcasted_iota(jnp.int32, (CHUNK, CHUNK), 1)
    causal = (c >= r) if rev else (c <= r)
    tri = causal.astype(BF16)
    xr = r ^ c
    m1 = causal & (xr < 16)
    m2 = causal & (xr >= 16) & (xr < 32)
    m3 = causal & (xr >= 32) & (xr < 64)
    m4 = causal & (xr >= 64)
    ref_rows = ((16, 15), (32, 16), (64, 32), (128, 64)) if rev else \
               ((16, 0), (32, 15), (64, 31), (128, 63))

    def block_row(b, blk, row):
        b3 = b.reshape(CHUNK // blk, blk, LANES)
        return jnp.broadcast_to(b3[:, row:row + 1, :], b3.shape).reshape(CHUNK, LANES)

    def chunk_body(ci, carry):
        cidx = (nchunk - 1 - ci) if rev else ci
        rows = pl.ds(pl.multiple_of(cidx * CHUNK, CHUNK), CHUNK)
        for h in range(HG_HEADS):
            cols = slice(h * LANES, (h + 1) * LANES)
            z = z_ref[0, rows, cols]
            qr = q_ref[0, rows, cols]
            v = i_ref[0, rows, cols]
            ra = 2 if rev else 0
            log_lb = lbp_ref[ra:ra + 1, cols]
            log_1mlb = lbp_ref[ra + 1:ra + 2, cols]

            sp = jnp.log(1.0 + jnp.exp(-jnp.abs(z)))
            ls = jnp.minimum(z, 0.0) - sp
            lsn = jnp.minimum(-z, 0.0) - sp
            ct = log_1mlb + ls
            mx = jnp.maximum(log_lb, ct)
            logf = mx + jnp.log(1.0 + jnp.exp(-jnp.abs(log_lb - ct)))
            k = jnp.exp(log_1mlb + lsn)
            q = qr / (1.0 + jnp.exp(-qr))

            l0 = logf.astype(BF16)
            r0 = logf - l0.astype(F32)
            l1 = r0.astype(BF16)
            l2 = (r0 - l1.astype(F32)).astype(BF16)
            b = (jnp.dot(tri, l0, preferred_element_type=F32)
                 + jnp.dot(tri, l1, preferred_element_type=F32)
                 + jnp.dot(tri, l2, preferred_element_type=F32))

            scores = []
            for lvl, (blk, row) in enumerate(ref_rows):
                rb = block_row(b, blk, row)
                dq = b - rb
                dk = rb - b
                if lvl > 0:
                    dq = jnp.minimum(dq, 0.0)
                    dk = jnp.minimum(dk, 0.0)
                qs = (q * jnp.exp(dq)).astype(BF16)
                ks = (k * jnp.exp(dk)).astype(BF16)
                scores.append(lax.dot_general(qs, ks, _NT, preferred_element_type=F32))
            a = jnp.where(m1, scores[0],
                          jnp.where(m2, scores[1],
                                    jnp.where(m3, scores[2],
                                              jnp.where(m4, scores[3], 0.0))))

            st = st_ref[h]
            vb = v.astype(BF16)
            o = jnp.dot(a.astype(BF16), vb, preferred_element_type=F32)
            qi = (q * jnp.exp(b)).astype(BF16)
            o = o + lax.dot_general(qi, st.astype(BF16), _NT, preferred_element_type=F32)

            btot = b[0:1, :] if rev else b[CHUNK - 1:CHUNK, :]
            kd = (k * jnp.exp(btot - b)).astype(BF16)
            st_ref[h] = st * jnp.exp(btot) + lax.dot_general(vb, kd, _TN, preferred_element_type=F32)

            if rev:
                tot = of_ref[0, rows, cols] + o
                ms = jnp.mean(tot * tot, axis=-1, keepdims=True)
                y = tot * lax.rsqrt(ms + NORM_EPS) * nw_ref[0:1, cols]
                g = g_ref[0, rows, cols]
                y = y * (g / (1.0 + jnp.exp(-g)))
                out_ref[0, rows, cols] = y.astype(out_ref.dtype)
            else:
                out_ref[0, rows, cols] = o
        return carry

    lax.fori_loop(0, nchunk, chunk_body, 0)


def _hgrn(proj3, lbp_l, norm_w_l, tb):
    bsz, seq, _ = proj3.shape
    nblk = seq // tb
    nchunk = tb // CHUNK

    def col_spec(col, rev):
        if rev:
            return pl.BlockSpec((1, tb, HG_WIDTH), lambda b, j: (b, nblk - 1 - j, col))
        return pl.BlockSpec((1, tb, HG_WIDTH), lambda b, j: (b, j, col))

    lbp_spec = pl.BlockSpec((4, HG_WIDTH), lambda b, j: (0, 0))
    scratch = [pltpu.VMEM((HG_HEADS, HG_DK, HG_DK), F32)]
    o_f = pl.pallas_call(
        functools.partial(_hgrn_kernel, rev=False, nchunk=nchunk),
        out_shape=jax.ShapeDtypeStruct((bsz, seq, HG_WIDTH), F32),
        grid=(bsz, nblk),
        in_specs=[col_spec(0, False), col_spec(1, False), col_spec(3, False), lbp_spec],
        out_specs=pl.BlockSpec((1, tb, HG_WIDTH), lambda b, j: (b, j, 0)),
        scratch_shapes=scratch,
        compiler_params=_cparams(("arbitrary", "arbitrary")),
        name="hgrn_fwd",
    )(proj3, proj3, proj3, lbp_l)
    rev_blk = pl.BlockSpec((1, tb, HG_WIDTH), lambda b, j: (b, nblk - 1 - j, 0))
    return pl.pallas_call(
        functools.partial(_hgrn_kernel, rev=True, nchunk=nchunk),
        out_shape=jax.ShapeDtypeStruct((bsz, seq, HG_WIDTH), BF16),
        grid=(bsz, nblk),
        in_specs=[col_spec(0, True), col_spec(2, True), col_spec(3, True), lbp_spec,
                  col_spec(4, True), rev_blk,
                  pl.BlockSpec((1, HG_WIDTH), lambda b, j: (0, 0))],
        out_specs=rev_blk,
        scratch_shapes=scratch,
        compiler_params=_cparams(("arbitrary", "arbitrary")),
        name="hgrn_bwd",
    )(proj3, proj3, proj3, lbp_l, proj3, o_f, norm_w_l)


def _attn_kernel(q_ref, kp_ref, kc_ref, kn_ref, vp_ref, vc_ref, vn_ref, o_ref, l_ref, *, tq, length):
    i = pl.program_id(2)
    tk = tq + 2 * ATTN_RADIUS
    q = q_ref[0]
    kf = jnp.concatenate([kp_ref[0], kc_ref[0], kn_ref[0]], axis=0).astype(BF16)
    vf = jnp.concatenate([vp_ref[0], vc_ref[0], vn_ref[0]], axis=0).astype(BF16)
    qpos = i * tq + lax.broadcasted_iota(jnp.int32, (tq, tk), 0)
    kpos = i * tq - ATTN_RADIUS + lax.broadcasted_iota(jnp.int32, (tq, tk), 1)
    valid = (jnp.abs(qpos - kpos) <= ATTN_RADIUS) & (kpos >= 0) & (kpos < length)
    lane = lax.broadcasted_iota(jnp.int32, (tq, LANES), 1)
    outs = []
    lses = []
    for h in range(2):
        qh = jnp.where((lane // HEAD_DIM) == h, q, 0.0).astype(BF16)
        s = lax.dot_general(qh, kf, _NT, preferred_element_type=F32)
        s = jnp.where(valid, s, NEG_BIG)
        m = jnp.max(s, axis=-1, keepdims=True)
        p = jnp.exp(s - m)
        l = jnp.sum(p, axis=-1, keepdims=True)
        pv = jnp.dot(p.astype(BF16), vf, preferred_element_type=F32)
        outs.append(pv / l)
        lses.append(jnp.broadcast_to(m + jnp.log(l), (tq, LANES)))
    first = lane < HEAD_DIM
    o_ref[0] = jnp.where(first, outs[0], outs[1])
    l_ref[0] = jnp.where(first, lses[0], lses[1])


def _attn_group(proj3, g):
    bsz, seq, _ = proj3.shape
    d = ATTN_DILATIONS[g]
    length = seq // d
    tq = min(length, 256)
    nq = length // tq
    hb = tq // ATTN_RADIUS
    nhb = length // ATTN_RADIUS
    pv = proj3.reshape(bsz, length, d * IN_COLS)
    qc = ATTN_COL0 // LANES + g
    kc = qc + ATTN_GROUPS
    vc = kc + ATTN_GROUPS

    def cur(col):
        return pl.BlockSpec((1, tq, LANES), lambda b, r, i: (b, i, r * COL_BLOCKS + col))

    def prev(col):
        return pl.BlockSpec((1, ATTN_RADIUS, LANES),
                            lambda b, r, i: (b, jnp.maximum(i * hb - 1, 0), r * COL_BLOCKS + col))

    def nxt(col):
        return pl.BlockSpec((1, ATTN_RADIUS, LANES),
                            lambda b, r, i: (b, jnp.minimum((i + 1) * hb, nhb - 1), r * COL_BLOCKS + col))

    out_spec = pl.BlockSpec((1, tq, LANES), lambda b, r, i: (b, i, r))
    o, lse = pl.pallas_call(
        functools.partial(_attn_kernel, tq=tq, length=length),
        out_shape=[jax.ShapeDtypeStruct((bsz, length, d * LANES), F32)] * 2,
        grid=(bsz, d, nq),
        in_specs=[cur(qc), prev(kc), cur(kc), nxt(kc), prev(vc), cur(vc), nxt(vc)],
        out_specs=[out_spec, out_spec],
        compiler_params=_cparams(("arbitrary", "arbitrary", "arbitrary")),
        name=f"attn_d{d}",
    )(pv, pv, pv, pv, pv, pv, pv)
    return o.reshape(bsz * seq, LANES), lse.reshape(bsz * seq, LANES)


def _outproj_kernel(x_ref, hg_ref, o0, o1, o2, l0, l1, l2, w_ref, nw_ref, wrt_ref,
                    xo_ref, h2_ref, lg_ref):
    ls = [l0[...], l1[...], l2[...]]
    mx = jnp.maximum(jnp.maximum(ls[0], ls[1]), ls[2])
    es = [jnp.exp(l - mx) for l in ls]
    den = es[0] + es[1] + es[2]
    parts = [hg_ref[...]]
    for o, e in zip((o0, o1, o2), es):
        parts.append((o[...] * (e / den)).astype(BF16))
    mix = jnp.concatenate(parts, axis=1)
    xn = x_ref[...] + jnp.dot(mix, w_ref[...], preferred_element_type=F32)
    xo_ref[...] = xn
    ms = jnp.mean(xn * xn, axis=-1, keepdims=True)
    h2 = xn * lax.rsqrt(ms + NORM_EPS) * nw_ref[...]
    h2_ref[...] = h2
    lg_ref[...] = lax.dot_general(wrt_ref[...], h2.astype(BF16), _NT, preferred_element_type=F32)


def _outproj(x2, hg2, attn_parts, w_bf, nw, wrt_bf, tm):
    n = x2.shape[0]
    row = lambda w: pl.BlockSpec((tm, w), lambda i: (i, 0))
    full = lambda a: pl.BlockSpec(a.shape, lambda i: (0,) * a.ndim)
    os_, ls_ = zip(*attn_parts)
    return pl.pallas_call(
        _outproj_kernel,
        out_shape=[jax.ShapeDtypeStruct((n, D_MODEL), F32),
                   jax.ShapeDtypeStruct((n, D_MODEL), F32),
                   jax.ShapeDtypeStruct((N_EXPERTS, n), F32)],
        grid=(n // tm,),
        in_specs=[row(D_MODEL), row(HG_WIDTH)] + [row(LANES)] * 6 + [full(w_bf), full(nw), full(wrt_bf)],
        out_specs=[row(D_MODEL), row(D_MODEL), pl.BlockSpec((N_EXPERTS, tm), lambda i: (0, i))],
        compiler_params=_cparams(("arbitrary",)),
        name="outproj",
    )(x2, hg2, *os_, *ls_, w_bf, nw, wrt_bf)


def _route_kernel(lg_ref, idx_ref, *, nt, cap):
    e = pl.program_id(0)
    lg = lg_ref[...]
    m = jnp.max(lg, axis=0)
    den = jnp.sum(jnp.exp(lg - m[None]), axis=0)
    aff = jnp.exp(lg_ref[e] - m) / den
    bits = pltpu.bitcast(aff, jnp.int32)

    def count(pred):
        s = jnp.sum(pred.astype(F32), axis=0, keepdims=True)
        return jnp.sum(s, axis=1, keepdims=True)

    capf = float(cap)

    def bisect(it, lo):
        cand = lo | jnp.left_shift(jnp.int32(1), 30 - it)
        return jnp.where(count(bits >= cand) >= capf, cand, lo)

    thr = lax.fori_loop(0, 31, bisect, jnp.zeros((1, 1), jnp.int32))
    gt = bits > thr
    eq = bits == thr
    need = capf - count(gt)

    rl = lax.broadcasted_iota(jnp.int32, (LANES, LANES), 0)
    cl = lax.broadcasted_iota(jnp.int32, (LANES, LANES), 1)
    upper = (rl <= cl).astype(BF16)
    ones = jnp.ones((LANES, LANES), BF16)
    rt_ = lax.broadcasted_iota(jnp.int32, (nt, nt), 0)
    ct_ = lax.broadcasted_iota(jnp.int32, (nt, nt), 1)
    lower_strict = (ct_ < rt_).astype(BF16)
    upper_nt = (rt_ <= ct_).astype(BF16)

    def prefix(mask_bf):
        within = jnp.dot(mask_bf, upper, preferred_element_type=F32)
        rowtot = jnp.dot(mask_bf, ones, preferred_element_type=F32)
        off = jnp.dot(lower_strict, rowtot.astype(BF16), preferred_element_type=F32)
        return within, off

    eq_bf = eq.astype(BF16)
    w_eq, off_eq = prefix(eq_bf)
    rank_eq = w_eq - eq_bf.astype(F32) + off_eq
    sel = gt | (eq & (rank_eq < need))
    sel_bf = sel.astype(BF16)
    within, off = prefix(sel_bf)

    ones8 = jnp.ones((8, LANES), BF16)
    rowtot_r = lax.dot_general(ones8, sel_bf, _NT, preferred_element_type=F32)
    incl_r = jnp.dot(rowtot_r.astype(BF16), upper_nt, preferred_element_type=F32)
    excl_r = incl_r - rowtot_r
    cio = lax.broadcasted_iota(jnp.int32, (cap, nt), 0).astype(F32)
    g = ((excl_r[0:1] <= cio) & (cio < incl_r[0:1])).astype(BF16)
    w_rows = jnp.dot(g, within.astype(BF16), preferred_element_type=F32)
    off_hi = jnp.floor(off * (1.0 / 256.0))
    off_lo = off - 256.0 * off_hi
    offc = (256.0 * jnp.dot(g, off_hi.astype(BF16), preferred_element_type=F32)
            + jnp.dot(g, off_lo.astype(BF16), preferred_element_type=F32))
    local = lax.broadcasted_iota(jnp.int32, (cap, LANES), 0).astype(F32) - offc
    before = (w_rows <= local).astype(BF16)
    lane_idx = lax.dot_general(ones8, before, _NT, preferred_element_type=F32)
    tvec = lax.broadcasted_iota(jnp.int32, (8, nt), 1).astype(BF16)
    tile_idx = lax.dot_general(tvec, g, _NT, preferred_element_type=F32)
    idx = (tile_idx * float(LANES) + lane_idx).astype(jnp.int32)
    idx_ref[0] = idx[0:1, :]


def _route(logits_t):
    n = logits_t.shape[1]
    nt = n // LANES
    assert nt <= 256, "tile ids must stay exact in bf16"
    cap = EC_CAPACITY * n // N_EXPERTS
    lg3 = logits_t.reshape(N_EXPERTS, nt, LANES)
    idx = pl.pallas_call(
        functools.partial(_route_kernel, nt=nt, cap=cap),
        out_shape=jax.ShapeDtypeStruct((N_EXPERTS, 1, cap), jnp.int32),
        grid=(N_EXPERTS,),
        in_specs=[pl.BlockSpec((N_EXPERTS, nt, LANES), lambda e: (0, 0, 0))],
        out_specs=pl.BlockSpec((1, 1, cap), lambda e: (e, 0, 0)),
        compiler_params=_cparams(("arbitrary",)),
        name="route",
    )(lg3)
    return idx.reshape(N_EXPERTS * cap)


def _moe_kernel(idx_ref, h2_hbm, x_hbm, wg_ref, wu_ref, wd_ref, wr_ref, xo_hbm,
                hbuf, xbuf, obuf, hsem, xsem, ssem, *, tc, nct):
    del x_hbm
    e = pl.program_id(0)
    c = pl.program_id(1)
    s = e * nct + c
    slot = s % 2
    nsteps = N_EXPERTS * nct

    def row_copies(step, sl, gather_h, gather_x, scatter):
        base = step * tc

        def body(j, carry):
            tok = idx_ref[base + j]
            if gather_h:
                pltpu.make_async_copy(h2_hbm.at[pl.ds(tok, 1)], hbuf.at[sl, pl.ds(j, 1)],
                                      hsem.at[sl]).start()
            if gather_x:
                pltpu.make_async_copy(xo_hbm.at[pl.ds(tok, 1)], xbuf.at[sl, pl.ds(j, 1)],
                                      xsem.at[sl]).start()
            if scatter:
                pltpu.make_async_copy(obuf.at[sl, pl.ds(j, 1)], xo_hbm.at[pl.ds(tok, 1)],
                                      ssem.at[sl]).start()
            return carry

        lax.fori_loop(0, tc, body, 0, unroll=8)

    def wait_h(sl):
        pltpu.make_async_copy(h2_hbm.at[pl.ds(0, tc)], hbuf.at[sl], hsem.at[sl]).wait()

    def wait_x(sl):
        pltpu.make_async_copy(xo_hbm.at[pl.ds(0, tc)], xbuf.at[sl], xsem.at[sl]).wait()

    def wait_s(sl):
        pltpu.make_async_copy(obuf.at[sl], xo_hbm.at[pl.ds(0, tc)], ssem.at[sl]).wait()

    @pl.when(s == 0)
    def _():
        row_copies(0, 0, True, True, False)

    wait_h(slot)
    wait_x(slot)

    @pl.when(s + 1 < nsteps)
    def _():
        row_copies(s + 1, 1 - slot, True, False, False)

    @pl.when(c + 1 < nct)
    def _():
        row_copies(s + 1, 1 - slot, False, True, False)

    @pl.when(c >= 2)
    def _():
        wait_s(slot)

    h = hbuf[slot].astype(BF16)
    lr = jnp.dot(h, wr_ref[...], preferred_element_type=F32)
    lane = lax.broadcasted_iota(jnp.int32, lr.shape, 1)
    lr = jnp.where(lane < N_EXPERTS, lr, NEG_BIG)
    er = jnp.exp(lr - jnp.max(lr, axis=-1, keepdims=True))
    aff = er / jnp.sum(er, axis=-1, keepdims=True)
    gate = jnp.sum(jnp.where(lane == e, aff, 0.0), axis=-1, keepdims=True)
    g = jnp.dot(h, wg_ref[0], preferred_element_type=F32)
    u = jnp.dot(h, wu_ref[0], preferred_element_type=F32)
    hm = ((g / (1.0 + jnp.exp(-g))) * u).astype(BF16)
    ye = jnp.dot(hm, wd_ref[0], preferred_element_type=F32)
    obuf[slot] = xbuf[slot] + ye * gate

    row_copies(s, slot, False, False, True)

    @pl.when(c == nct - 1)
    def _():
        wait_s(slot)

        @pl.when(c >= 1)
        def _():
            wait_s(1 - slot)

        @pl.when(s + 1 < nsteps)
        def _():
            row_copies(s + 1, 1 - slot, False, True, False)


def _moe(idx_flat, h2, x2, wg_bf, wu_bf, wd_bf, wr_pad_bf, tc=512):
    n = x2.shape[0]
    cap = EC_CAPACITY * n // N_EXPERTS
    tc = min(cap, tc)
    nct = cap // tc
    any_spec = pl.BlockSpec(memory_space=pl.ANY)
    wspec = pl.BlockSpec((1, D_MODEL, D_MODEL), lambda e, c, idx: (e, 0, 0))
    grid_spec = pltpu.PrefetchScalarGridSpec(
        num_scalar_prefetch=1,
        grid=(N_EXPERTS, nct),
        in_specs=[any_spec, any_spec, wspec, wspec, wspec,
                  pl.BlockSpec((D_MODEL, LANES), lambda e, c, idx: (0, 0))],
        out_specs=any_spec,
        scratch_shapes=[pltpu.VMEM((2, tc, D_MODEL), F32),
                        pltpu.VMEM((2, tc, D_MODEL), F32),
                        pltpu.VMEM((2, tc, D_MODEL), F32),
                        pltpu.SemaphoreType.DMA((2,)),
                        pltpu.SemaphoreType.DMA((2,)),
                        pltpu.SemaphoreType.DMA((2,))],
    )
    return pl.pallas_call(
        functools.partial(_moe_kernel, tc=tc, nct=nct),
        out_shape=jax.ShapeDtypeStruct((n, D_MODEL), F32),
        grid_spec=grid_spec,
        input_output_aliases={2: 0},
        compiler_params=_cparams(("arbitrary", "arbitrary")),
        name="moe",
    )(idx_flat, h2, x2, wg_bf, wu_bf, wd_bf, wr_pad_bf)


def _norm_kernel(x_ref, w_ref, o_ref):
    x = x_ref[...]
    ms = jnp.mean(x * x, axis=-1, keepdims=True)
    o_ref[...] = x * lax.rsqrt(ms + NORM_EPS) * w_ref[...]


def _final_norm(x2, w, tm):
    n = x2.shape[0]
    return pl.pallas_call(
        _norm_kernel,
        out_shape=jax.ShapeDtypeStruct((n, D_MODEL), F32),
        grid=(n // tm,),
        in_specs=[pl.BlockSpec((tm, D_MODEL), lambda i: (i, 0)),
                  pl.BlockSpec((1, D_MODEL), lambda i: (0, 0))],
        out_specs=pl.BlockSpec((tm, D_MODEL), lambda i: (i, 0)),
        compiler_params=_cparams(("arbitrary",)),
        name="final_norm",
    )(x2, w)


def _tiles(seq):
    tm = min(seq, 256)
    tb = min(seq, 512)
    return tm, tb


def _trunk(x, params):
    bsz, seq, _ = x.shape
    n = bsz * seq
    tm, tb = _tiles(seq)
    cos, sin = _rotary_tables(seq)
    x2 = x.reshape(n, D_MODEL)
    for l in range(DEPTH):
        proj = _inproj(x2, params["norm1_w"][l:l + 1], params["w_in"][l], cos, sin, seq, tm)
        proj3 = proj.reshape(bsz, seq, IN_COLS)
        hg = _hgrn(proj3, params["lbp"][l], params["hgrn_norm_w"][l:l + 1], tb)
        attn_parts = [_attn_group(proj3, g) for g in range(ATTN_GROUPS)]
        x2, h2, logits_t = _outproj(x2, hg.reshape(n, HG_WIDTH), attn_parts, params["w_out"][l],
                                    params["norm2_w"][l:l + 1], params["w_router_t"][l], tm)
        idx = _route(logits_t)
        x2 = _moe(idx, h2, x2, params["w_gate"][l], params["w_up"][l], params["w_down"][l],
                  params["w_router_pad"][l])
    y = _final_norm(x2, params["final_norm_w"], tm)
    return y.reshape(bsz, seq, D_MODEL)


def kernel(x_prompt, x_sample, norm1_w, w_in, lb_fwd_logits, lb_bwd_logits, hgrn_norm_w, w_out,
           norm2_w, w_router, w_gate, w_up, w_down, final_norm_w):
    w_router_bf = w_router.astype(BF16)
    params = {
        "norm1_w": norm1_w,
        "w_in": w_in.astype(BF16),
        "lbp": _lower_bound_logs(lb_fwd_logits, lb_bwd_logits),
        "hgrn_norm_w": hgrn_norm_w,
        "w_out": w_out.astype(BF16),
        "norm2_w": norm2_w,
        "w_router_t": jnp.swapaxes(w_router_bf, 1, 2),
        "w_router_pad": jnp.pad(w_router_bf, ((0, 0), (0, 0), (0, LANES - N_EXPERTS))),
        "w_gate": w_gate.astype(BF16),
        "w_up": w_up.astype(BF16),
        "w_down": w_down.astype(BF16),
        "final_norm_w": final_norm_w.reshape(1, D_MODEL),
    }
    return (_trunk(x_prompt, params), _trunk(x_sample, params))
```
